```python
import math
import numpy as np
import jax
import jax.numpy as jnp
from jax import lax

D_MODEL = 2048
BATCH = 8
SEQ = 4096
DEPTH = 4

CHUNK = 64
NORM_EPS = 1e-6
MASK_VALUE = -1e30
LOG_FLOOR = 1e-30
CONV_DIM = 1024
CONV_WIDTH = 31
DA_HEADS = 8
DA_HEAD_DIM = 64
DA_V_DIM = 2 * DA_HEAD_DIM
ROT_DIM = DA_HEAD_DIM // 4
ROPE_THETA = 500000.0
Q_BLOCK = 128
HG_HEADS = 8
HG_KEY_DIM = 128
HG_VAL_DIM = 128
N_BRANCH = 3
N_GROUPS = 4
EXPERTS_PER_GROUP = 8
N_EXPERTS = N_GROUPS * EXPERTS_PER_GROUP
EXPERT_HIDDEN = 512
TOP_K = 2
ROW_BLOCK = 256
DA_QK_COLS = DA_HEADS * 2 * DA_HEAD_DIM
DA_V_COLS = DA_HEADS * DA_V_DIM
HG_K_COLS = HG_HEADS * HG_KEY_DIM
HG_V_COLS = HG_HEADS * HG_VAL_DIM
GATE_COLS = N_BRANCH * D_MODEL
SPLIT_SIZES = (2 * CONV_DIM, DA_QK_COLS, DA_QK_COLS, DA_V_COLS, HG_K_COLS, HG_K_COLS, HG_V_COLS, HG_V_COLS, GATE_COLS)
IN_COLS = sum(SPLIT_SIZES)
ADA_COLS = 6 * D_MODEL

kernel_name = 'hybrid_chunk_causal_encoder_trunk'


def rms_norm(x, gain):
    xf = x.astype(jnp.float32)
    y = xf * lax.rsqrt(jnp.mean(xf * xf, axis=-1, keepdims=True) + NORM_EPS)
    return y.astype(x.dtype) * gain


def layer_norm(x, gain, bias):
    xf = x.astype(jnp.float32)
    mu = jnp.mean(xf, axis=-1, keepdims=True)
    xc = xf - mu
    y = xc * lax.rsqrt(jnp.mean(xc * xc, axis=-1, keepdims=True) + NORM_EPS)
    return y.astype(x.dtype) * gain + bias


def modulate(h, shift, scale):
    return h * (1 + scale[:, None, :]) + shift[:, None, :]


def rope_tables(positions, dtype):
    inv_freq = ROPE_THETA ** (-jnp.arange(0, ROT_DIM, 2, dtype=jnp.float32) / ROT_DIM)
    ang = positions.astype(jnp.float32)[..., None] * inv_freq
    return jnp.cos(ang).astype(dtype), jnp.sin(ang).astype(dtype)


def apply_partial_rope(t, cos, sin):
    cos = cos[:, :, None, None, :]
    sin = sin[:, :, None, None, :]
    half = ROT_DIM // 2
    t1 = t[..., :half]
    t2 = t[..., half:ROT_DIM]
    rot = jnp.concatenate([t1 * cos - t2 * sin, t2 * cos + t1 * sin], axis=-1)
    return jnp.concatenate([rot, t[..., ROT_DIM:]], axis=-1)


def conformer_conv(a_in, w_dw, b_dw, ln_g, ln_b, w_out):
    u = a_in[..., :CONV_DIM] * jax.nn.sigmoid(a_in[..., CONV_DIM:])
    u = jnp.pad(u, ((0, 0), (CONV_WIDTH - 1, 0), (0, 0)))
    u = lax.conv_general_dilated(u, w_dw[:, None, :], window_strides=(1,), padding='VALID',
                                 dimension_numbers=('NWC', 'WIO', 'NWC'),
                                 feature_group_count=CONV_DIM) + b_dw
    u = jax.nn.silu(layer_norm(u, ln_g, ln_b))
    return u @ w_out


def diff_attention(q, k, v, lam, lam_init, subln_g, w_out):
    bsz, seq = q.shape[0], q.shape[1]
    scale = DA_HEAD_DIM ** -0.5
    outs = []
    for start in range(0, seq, Q_BLOCK):
        end = start + Q_BLOCK
        qb = q[:, start:end]
        kb = k[:, :end]
        vb = v[:, :end]
        s = jnp.einsum('bqhcd,bkhcd->bhcqk', qb, kb, preferred_element_type=jnp.float32) * scale
        q_chunk = (start + jnp.arange(Q_BLOCK)) // CHUNK
        k_chunk = jnp.arange(end) // CHUNK
        mask = k_chunk[None, :] <= q_chunk[:, None]
        p = jax.nn.softmax(jnp.where(mask, s, MASK_VALUE), axis=-1)
        a = p[:, :, 0] - lam * p[:, :, 1]
        outs.append(jnp.einsum('bhqk,bkhe->bqhe', a.astype(v.dtype), vb))
    o = jnp.concatenate(outs, axis=1)
    o = rms_norm(o, subln_g) * (1 - lam_init)
    return o.reshape(bsz, seq, DA_HEADS * DA_V_DIM) @ w_out


def hgrn2(q_raw, f_raw, i_raw, g_raw, lb, gnorm_g, w_out):
    bsz, seq = q_raw.shape[0], q_raw.shape[1]
    dtype = i_raw.dtype
    f32 = jnp.float32
    q = jax.nn.silu(q_raw).reshape(bsz, seq, HG_HEADS, HG_KEY_DIM).astype(f32)
    fr = f_raw.reshape(bsz, seq, HG_HEADS, HG_KEY_DIM).astype(f32)
    lbh = lb.reshape(HG_HEADS, HG_KEY_DIM).astype(f32)
    f = lbh + (1 - lbh) * jax.nn.sigmoid(fr)
    log_f = jnp.log(jnp.maximum(f, LOG_FLOOR))
    kk = (1 - lbh) * jax.nn.sigmoid(-fr)
    vv = i_raw.reshape(bsz, seq, HG_HEADS, HG_VAL_DIM).astype(f32)
    n_chunks = seq // CHUNK

    def to_chunks(t):
        return t.reshape(bsz, n_chunks, CHUNK, HG_HEADS, -1).transpose(1, 0, 3, 2, 4)

    causal = jnp.tril(jnp.ones((CHUNK, CHUNK), dtype=bool))[None, None, :, :, None]

    def step(state, inp):
        qc, kc, vc, lfc = inp
        b = jnp.cumsum(lfc, axis=2)
        o_inter = jnp.einsum('bhtk,bhkv->bhtv', qc * jnp.exp(b), state)
        diff = b[:, :, :, None, :] - b[:, :, None, :, :]
        decay = jnp.where(causal, jnp.exp(jnp.where(causal, diff, 0.0)), 0.0)
        scores = jnp.einsum('bhtk,bhsk,bhtsk->bhts', qc, kc, decay)
        o_intra = jnp.einsum('bhts,bhsv->bhtv', scores, vc)
        b_last = b[:, :, -1:, :]
        new_state = jnp.exp(b_last[:, :, 0, :])[..., None] * state + jnp.einsum(
            'bhsk,bhsv->bhkv', kc * jnp.exp(b_last - b), vc)
        return new_state, o_inter + o_intra

    init = jnp.zeros((bsz, HG_HEADS, HG_KEY_DIM, HG_VAL_DIM), f32)
    _, o = lax.scan(step, init, (to_chunks(q), to_chunks(kk), to_chunks(vv), to_chunks(log_f)))
    o = o.transpose(1, 0, 3, 2, 4).reshape(bsz, seq, HG_HEADS, HG_VAL_DIM).astype(dtype)
    o = rms_norm(o, gnorm_g) * jax.nn.silu(g_raw.reshape(bsz, seq, HG_HEADS, HG_VAL_DIM))
    return o.reshape(bsz, seq, HG_HEADS * HG_VAL_DIM) @ w_out


def expert_dispatch(xt, e_idx, e_w, w_gate, w_up, w_down):
    n_tok, d = xt.shape
    n_assign = n_tok * TOP_K
    flat_e = e_idx.reshape(n_assign)
    flat_w = e_w.reshape(n_assign)
    flat_tok = jnp.repeat(jnp.arange(n_tok, dtype=jnp.int32), TOP_K)
    order = jnp.argsort(flat_e)
    se = flat_e[order]
    stok = flat_tok[order]
    sw = flat_w[order]
    counts = jnp.bincount(flat_e, length=N_EXPERTS)
    padded = ((counts + ROW_BLOCK - 1) // ROW_BLOCK) * ROW_BLOCK
    pad_end = jnp.cumsum(padded)
    pad_start = pad_end - padded
    start = jnp.cumsum(counts) - counts
    dest = pad_start[se] + jnp.arange(n_assign, dtype=jnp.int32) - start[se]
    n_blocks = -(-(n_assign + N_EXPERTS * (ROW_BLOCK - 1)) // ROW_BLOCK)
    n_slots = n_blocks * ROW_BLOCK
    slot_tok = jnp.full((n_slots,), n_tok, dtype=jnp.int32).at[dest].set(stok)
    slot_w = jnp.zeros((n_slots,), jnp.float32).at[dest].set(sw)
    block_exp = jnp.minimum(jnp.searchsorted(pad_end, jnp.arange(n_blocks, dtype=jnp.int32) * ROW_BLOCK,
                                             side='right'), N_EXPERTS - 1)
    x_pad = jnp.concatenate([xt, jnp.zeros((1, d), xt.dtype)], axis=0)
    xb = x_pad[slot_tok].reshape(n_blocks, ROW_BLOCK, d)

    def expert_block(args):
        xblk, e = args
        hid = jax.nn.silu(xblk @ w_gate[e]) * (xblk @ w_up[e])
        return hid @ w_down[e]

    yb = lax.map(expert_block, (xb, block_exp)).reshape(n_slots, d)
    y = jnp.zeros((n_tok + 1, d), jnp.float32).at[slot_tok].add(yb.astype(jnp.float32) * slot_w[:, None])
    return y[:n_tok].astype(xt.dtype)


def hier_moe(h, w_group, b_group, w_route, b_route, w_gate, w_up, w_down):
    bsz, seq, d = h.shape
    xt = h.reshape(bsz * seq, d)
    g_logits = (xt @ w_group).astype(jnp.float32) + b_group
    g_probs = jax.nn.softmax(g_logits, axis=-1)
    g_sel = jnp.argmax(g_logits, axis=-1).astype(jnp.int32)
    g_w = jnp.take_along_axis(g_probs, g_sel[:, None], axis=-1)
    e_logits = ((xt @ w_route).astype(jnp.float32) + b_route).reshape(-1, N_GROUPS, EXPERTS_PER_GROUP)
    e_logits = jnp.take_along_axis(e_logits, g_sel[:, None, None], axis=1)[:, 0]
    top_vals, top_idx = lax.top_k(e_logits, TOP_K)
    e_w = jax.nn.softmax(top_vals, axis=-1) * g_w
    e_idx = g_sel[:, None] * EXPERTS_PER_GROUP + top_idx.astype(jnp.int32)
    return expert_dispatch(xt, e_idx, e_w, w_gate, w_up, w_down).reshape(bsz, seq, d)


def setup_inputs(seed: int = 0) -> dict:
    key = jax.random.key(seed)
    ks = jax.random.split(key, 32)
    f32 = jnp.float32
    L, D = DEPTH, D_MODEL

    def nrm(k, shape, scale):
        return jax.random.normal(k, shape, f32) * scale

    x = nrm(ks[0], (BATCH, SEQ, D), 1.0)
    c = nrm(ks[1], (BATCH, D), 1.0)
    offsets = jax.random.randint(ks[2], (BATCH, 1), 0, 1024, dtype=jnp.int32) * CHUNK
    positions = offsets + jnp.arange(SEQ, dtype=jnp.int32)[None, :]
    return {
        'x': x,
        'c': c,
        'positions': positions,
        'norm1_g': 1.0 + nrm(ks[3], (L, D), 0.02),
        'norm2_g': 1.0 + nrm(ks[4], (L, D), 0.02),
        'w_ada': nrm(ks[5], (L, D, ADA_COLS), 0.5 * D ** -0.5),
        'b_ada': nrm(ks[6], (L, ADA_COLS), 0.01),
        'w_in': nrm(ks[7], (L, D, IN_COLS), D ** -0.5),
        'conv_dw': nrm(ks[8], (L, CONV_WIDTH, CONV_DIM), CONV_WIDTH ** -0.5),
        'conv_dw_b': nrm(ks[9], (L, CONV_DIM), 0.01),
        'conv_ln_g': 1.0 + nrm(ks[10], (L, CONV_DIM), 0.02),
        'conv_ln_b': nrm(ks[11], (L, CONV_DIM), 0.01),
        'w_conv_out': nrm(ks[12], (L, CONV_DIM, D), CONV_DIM ** -0.5),
        'da_lambda': nrm(ks[13], (L, 4, DA_HEAD_DIM), 0.1),
        'da_subln_g': 1.0 + nrm(ks[14], (L, DA_V_DIM), 0.02),
        'w_da_out': nrm(ks[15], (L, DA_V_COLS, D), DA_V_COLS ** -0.5),
        'hg_lb': nrm(ks[16], (L, HG_K_COLS), 0.1),
        'hg_norm_g': 1.0 + nrm(ks[17], (L, HG_VAL_DIM), 0.02),
        'w_hg_out': nrm(ks[18], (L, HG_V_COLS, D), HG_V_COLS ** -0.5),
        'w_o': nrm(ks[19], (L, D, D), D ** -0.5),
        'w_group': nrm(ks[20], (L, D, N_GROUPS), D ** -0.5),
        'b_group': nrm(ks[21], (L, N_GROUPS), 0.01),
        'w_route': nrm(ks[22], (L, D, N_EXPERTS), D ** -0.5),
        'b_route': nrm(ks[23], (L, N_EXPERTS), 0.01),
        'w_gate': nrm(ks[24], (L, N_EXPERTS, D, EXPERT_HIDDEN), D ** -0.5),
        'w_up': nrm(ks[25], (L, N_EXPERTS, D, EXPERT_HIDDEN), D ** -0.5),
        'w_down': nrm(ks[26], (L, N_EXPERTS, EXPERT_HIDDEN, D), EXPERT_HIDDEN ** -0.5),
        'final_g': 1.0 + nrm(ks[27], (D,), 0.02),
    }


def reference(x, c, positions, norm1_g, norm2_g, w_ada, b_ada, w_in, conv_dw, conv_dw_b, conv_ln_g,
              conv_ln_b, w_conv_out, da_lambda, da_subln_g, w_da_out, hg_lb, hg_norm_g, w_hg_out, w_o,
              w_group, b_group, w_route, b_route, w_gate, w_up, w_down, final_g):
    bsz, seq, d = x.shape
    cos, sin = rope_tables(positions, x.dtype)
    cond = jax.nn.silu(c)
    lb_p = jax.nn.softmax(hg_lb.astype(jnp.float32), axis=0)
    lower = jnp.cumsum(lb_p, axis=0) - lb_p[0]
    split_at = np.cumsum(SPLIT_SIZES)[:-1].tolist()
    for l in range(DEPTH):
        ada = cond @ w_ada[l] + b_ada[l]
        sh1, sc1, g1, sh2, sc2, g2 = jnp.split(ada, 6, axis=-1)
        h = modulate(rms_norm(x, norm1_g[l]), sh1, sc1)
        a_in, dq, dk, dv, hq, hf, hi, hg, gts = jnp.split(h @ w_in[l], split_at, axis=-1)
        y_a = conformer_conv(a_in, conv_dw[l], conv_dw_b[l], conv_ln_g[l], conv_ln_b[l], w_conv_out[l])
        q = apply_partial_rope(dq.reshape(bsz, seq, DA_HEADS, 2, DA_HEAD_DIM), cos, sin)
        k = apply_partial_rope(dk.reshape(bsz, seq, DA_HEADS, 2, DA_HEAD_DIM), cos, sin)
        v = dv.reshape(bsz, seq, DA_HEADS, DA_V_DIM)
        lam_init = 0.8 - 0.6 * math.exp(-0.3 * l)
        lv = da_lambda[l].astype(jnp.float32)
        lam = jnp.exp(jnp.sum(lv[0] * lv[1])) - jnp.exp(jnp.sum(lv[2] * lv[3])) + lam_init
        y_b = diff_attention(q, k, v, lam, lam_init, da_subln_g[l], w_da_out[l])
        y_c = hgrn2(hq, hf, hi, hg, lower[l], hg_norm_g[l], w_hg_out[l])
        gate = jax.nn.sigmoid(gts.reshape(bsz, seq, N_BRANCH, d))
        merged = gate[:, :, 0] * y_a + gate[:, :, 1] * y_b + gate[:, :, 2] * y_c
        x = x + g1[:, None, :] * (merged @ w_o[l])
        h2 = modulate(rms_norm(x, norm2_g[l]), sh2, sc2)
        x = x + g2[:, None, :] * hier_moe(h2, w_group[l], b_group[l], w_route[l], b_route[l],
                                          w_gate[l], w_up[l], w_down[l])
    return rms_norm(x, final_g)
```

```python
import functools
import math

import numpy as np
import jax
import jax.numpy as jnp
from jax import lax
from jax.experimental import pallas as pl
from jax.experimental.pallas import tpu as pltpu

F32 = jnp.float32
BF16 = jnp.bfloat16
HIGHEST = lax.Precision.HIGHEST

LANES = 128
CHUNK = 64
NORM_EPS = 1e-6
MASK_VALUE = -1e30
LOG_FLOOR = 1e-30
CONV_DIM = 1024
CONV_WIDTH = 31
CONV_HALO = 32
DA_HEADS = 8
DA_HEAD_DIM = 64
ROT_DIM = DA_HEAD_DIM // 4
ROPE_THETA = 500000.0
HG_HEADS = 8
HG_DIM = 128
N_BRANCH = 3
N_GROUPS = 4
EXPERTS_PER_GROUP = 8
N_EXPERTS = N_GROUPS * EXPERTS_PER_GROUP
TOP_K = 2
ROW_BLOCK = 256
VMEM_LIMIT = 56 * 1024 * 1024


def _cparams(*sem):
    return pltpu.CompilerParams(dimension_semantics=sem, vmem_limit_bytes=VMEM_LIMIT)


def _sigmoid(v):
    return 1.0 / (1.0 + jnp.exp(-v))


def _silu(v):
    return v * _sigmoid(v)


def _dot(a, b):
    return jnp.dot(a, b, preferred_element_type=F32)


def _dot_nt(a, b):
    return lax.dot_general(a, b, (((1,), (1,)), ((), ())), preferred_element_type=F32)


def _dot_tn(a, b):
    return lax.dot_general(a, b, (((0,), (0,)), ((), ())), preferred_element_type=F32)


def _rms_mod(x, gain, shift, scale):
    ms = jnp.mean(x * x, axis=-1, keepdims=True)
    y = x * lax.rsqrt(ms + NORM_EPS) * gain
    return y * (1.0 + scale) + shift


def _ada_kernel(c_ref, w_ref, b_ref, o_ref):
    cond = _silu(c_ref[...])
    o_ref[...] = jnp.dot(cond, w_ref[...], preferred_element_type=F32, precision=HIGHEST) + b_ref[...]


def _ada(c, w_ada, b_ada):
    depth, d, cols = w_ada.shape
    bsz = c.shape[0]
    tn = 1024
    return pl.pallas_call(
        _ada_kernel,
        grid=(depth, cols // tn),
        in_specs=[pl.BlockSpec((bsz, d), lambda l, j: (0, 0)),
                  pl.BlockSpec((None, d, tn), lambda l, j: (l, 0, j)),
                  pl.BlockSpec((None, 1, tn), lambda l, j: (l, 0, j))],
        out_specs=pl.BlockSpec((None, bsz, tn), lambda l, j: (l, 0, j)),
        out_shape=jax.ShapeDtypeStruct((depth, bsz, cols), F32),
        compiler_params=_cparams("parallel", "parallel"),
        name="ada",
    )(c, w_ada, b_ada.reshape(depth, 1, cols))


def _rope_kernel(pos_ref, freq_ref, sign_ref, cos_ref, sin_ref):
    ang = pos_ref[...].astype(F32) * freq_ref[...]
    rot = sign_ref[...] != 0.0
    cos_ref[...] = jnp.where(rot, jnp.cos(ang), 1.0)
    sin_ref[...] = jnp.sin(ang) * sign_ref[...]


def _rope_tables(positions):
    bsz, seq = positions.shape
    half = ROT_DIM // 2
    inv_freq = ROPE_THETA ** (-jnp.arange(0, ROT_DIM, 2, dtype=F32) / ROT_DIM)
    lane = np.arange(LANES) % DA_HEAD_DIM
    freq = jnp.where(lane < ROT_DIM, inv_freq[lane % half], 0.0).reshape(1, LANES).astype(F32)
    sign = np.where(lane < half, -1.0, np.where(lane < ROT_DIM, 1.0, 0.0)).reshape(1, LANES).astype(np.float32)
    spec = pl.BlockSpec((None, seq, LANES), lambda b: (b, 0, 0))
    return pl.pallas_call(
        _rope_kernel,
        grid=(bsz,),
        in_specs=[pl.BlockSpec((None, seq, 1), lambda b: (b, 0, 0)),
                  pl.BlockSpec((1, LANES), lambda b: (0, 0)),
                  pl.BlockSpec((1, LANES), lambda b: (0, 0))],
        out_specs=[spec, spec],
        out_shape=[jax.ShapeDtypeStruct((bsz, seq, LANES), F32)] * 2,
        compiler_params=_cparams("parallel"),
        name="rope_tables",
    )(positions.reshape(bsz, seq, 1), freq, jnp.asarray(sign))


def _rope(t, cos, sin):
    half = ROT_DIM // 2
    lane = lax.broadcasted_iota(jnp.int32, t.shape, 1) % DA_HEAD_DIM
    partner = jnp.where(lane < half, pltpu.roll(t, LANES - half, 1), pltpu.roll(t, half, 1))
    return t * cos + partner * sin


def _inproj_kernel(x_ref, g_ref, sh_ref, sc_ref, w_ref, o_ref, h_sc):
    @pl.when(pl.program_id(1) == 0)
    def _():
        h_sc[...] = _rms_mod(x_ref[...], g_ref[...], sh_ref[...], sc_ref[...]).astype(BF16)

    o_ref[...] = _dot(h_sc[...], w_ref[...]).astype(BF16)


def _in_proj(x2, gain, shift, scale, w_bf, seq):
    t, d = x2.shape
    cols = w_bf.shape[1]
    tm = min(1024, seq)
    tn = 1024
    per_b = seq // tm
    mod_spec = pl.BlockSpec((None, 1, d), lambda i, j: (i // per_b, 0, 0))
    return pl.pallas_call(
        _inproj_kernel,
        grid=(t // tm, cols // tn),
        in_specs=[pl.BlockSpec((tm, d), lambda i, j: (i, 0)),
                  pl.BlockSpec((1, d), lambda i, j: (0, 0)),
                  mod_spec, mod_spec,
                  pl.BlockSpec((d, tn), lambda i, j: (0, j))],
        out_specs=pl.BlockSpec((tm, tn), lambda i, j: (i, j)),
        out_shape=jax.ShapeDtypeStruct((t, cols), BF16),
        scratch_shapes=[pltpu.VMEM((tm, d), BF16)],
        compiler_params=_cparams("parallel", "arbitrary"),
        name="in_proj",
    )(x2, gain, shift, scale, w_bf)


def _conv_kernel(zc_ref, zh_ref, w_ref, b_ref, lg_ref, lb_ref, o_ref, u_sc, cv_sc):
    ts = zc_ref.shape[0]
    a = zc_ref[...].astype(F32)
    u_sc[CONV_HALO:, :] = a[:, :CONV_DIM] * _sigmoid(a[:, CONV_DIM:])
    ah = zh_ref[...].astype(F32)
    uh = ah[:, :CONV_DIM] * _sigmoid(ah[:, CONV_DIM:])
    u_sc[:CONV_HALO, :] = jnp.where(pl.program_id(1) == 0, 0.0, uh)
    rc, cc = 64, 256
    off = CONV_HALO - (CONV_WIDTH - 1)
    for r0 in range(0, ts, rc):
        for c0 in range(0, CONV_DIM, cc):
            acc = jnp.broadcast_to(b_ref[:, c0:c0 + cc], (rc, cc))
            for j in range(CONV_WIDTH):
                acc = acc + w_ref[j:j + 1, c0:c0 + cc] * u_sc[r0 + off + j:r0 + off + j + rc, c0:c0 + cc]
            cv_sc[r0:r0 + rc, c0:c0 + cc] = acc
    v = cv_sc[...]
    mu = jnp.mean(v, axis=-1, keepdims=True)
    vc = v - mu
    y = vc * lax.rsqrt(jnp.mean(vc * vc, axis=-1, keepdims=True) + NORM_EPS)
    o_ref[...] = _silu(y * lg_ref[...] + lb_ref[...]).astype(BF16)


def _conv_branch(z, conv_dw, conv_b, ln_g, ln_b, bsz, seq):
    t = z.shape[0]
    ts = min(256, seq)
    per_b = seq // ts
    hb = ts // CONV_HALO
    row = lambda v: v.reshape(1, CONV_DIM)
    const = lambda shp: pl.BlockSpec(shp, lambda b, i: (0, 0))
    return pl.pallas_call(
        _conv_kernel,
        grid=(bsz, per_b),
        in_specs=[pl.BlockSpec((ts, 2 * CONV_DIM), lambda b, i: (b * per_b + i, 0)),
                  pl.BlockSpec((CONV_HALO, 2 * CONV_DIM),
                               lambda b, i: (jnp.maximum((b * per_b + i) * hb - 1, 0), 0)),
                  const((CONV_WIDTH, CONV_DIM)), const((1, CONV_DIM)),
                  const((1, CONV_DIM)), const((1, CONV_DIM))],
        out_specs=pl.BlockSpec((ts, CONV_DIM), lambda b, i: (b * per_b + i, 0)),
        out_shape=jax.ShapeDtypeStruct((t, CONV_DIM), BF16),
        scratch_shapes=[pltpu.VMEM((ts + CONV_HALO, CONV_DIM), F32), pltpu.VMEM((ts, CONV_DIM), F32)],
        compiler_params=_cparams("parallel", "arbitrary"),
        name="conv_branch",
    )(z, z, conv_dw, row(conv_b), row(ln_g), row(ln_b))


ATT_TQ = 128
ATT_TK = 512


def _attn_kernel(li_ref, q_ref, k_ref, v_ref, cq_ref, sq_ref, ck_ref, sk_ref, dl_ref, g_ref, o_ref,
                 kr_sc, va_sc, acc_sc, m_sc, *, tk):
    qi = pl.program_id(2)
    tq = q_ref.shape[0]
    dv = v_ref.shape[1]

    @pl.when(qi == 0)
    def _():
        kr_sc[...] = _rope(k_ref[...].astype(F32), ck_ref[...], sk_ref[...]).astype(BF16)
        va_sc[:, :dv] = v_ref[...]
        va_sc[:, dv:] = jnp.ones((va_sc.shape[0], dv), BF16)

    q = _rope(q_ref[...].astype(F32), cq_ref[...], sq_ref[...]) * (DA_HEAD_DIM ** -0.5)
    lane = lax.broadcasted_iota(jnp.int32, q.shape, 1)
    qc = (jnp.where(lane < DA_HEAD_DIM, q, 0.0).astype(BF16),
          jnp.where(lane >= DA_HEAD_DIM, q, 0.0).astype(BF16))
    acc_sc[...] = jnp.zeros(acc_sc.shape, F32)
    m_sc[...] = jnp.full(m_sc.shape, MASK_VALUE, F32)

    def step(j, masked):
        k0 = pl.multiple_of(j * tk, tk)
        kb = kr_sc[pl.ds(k0, tk), :]
        vb = va_sc[pl.ds(k0, tk), :]
        if masked:
            qch = (qi * tq + lax.broadcasted_iota(jnp.int32, (tq, tk), 0)) // CHUNK
            kch = (k0 + lax.broadcasted_iota(jnp.int32, (tq, tk), 1)) // CHUNK
            vis = kch <= qch
        for c in range(2):
            s = _dot_nt(qc[c], kb)
            if masked:
                s = jnp.where(vis, s, MASK_VALUE)
            m_old = m_sc[c]
            m_new = jnp.maximum(m_old, jnp.max(s, axis=-1, keepdims=True))
            p = jnp.exp(s - m_new)
            acc_sc[c] = jnp.exp(m_old - m_new) * acc_sc[c] + _dot(p.astype(BF16), vb)
            m_sc[c] = m_new

    n_full = (qi * tq) // tk

    def body(j, carry):
        step(j, False)
        return carry

    lax.fori_loop(0, n_full, body, 0)
    step(n_full, True)

    lam_init = li_ref[0]
    lv = dl_ref[...]
    lam = (jnp.exp(jnp.sum(lv[0:1] * lv[1:2], axis=-1, keepdims=True))
           - jnp.exp(jnp.sum(lv[2:3] * lv[3:4], axis=-1, keepdims=True)) + lam_init)
    a0 = acc_sc[0]
    a1 = acc_sc[1]
    o = a0[:, :dv] / a0[:, dv:dv + 1] - lam * (a1[:, :dv] / a1[:, dv:dv + 1])
    o = o * lax.rsqrt(jnp.mean(o * o, axis=-1, keepdims=True) + NORM_EPS) * g_ref[...]
    o_ref[...] = (o * (1.0 - lam_init)).astype(BF16)


def _diff_attention(z, cos_t, sin_t, da_lambda, subln_g, lam_init, bsz, seq):
    t = z.shape[0]
    tq = min(ATT_TQ, seq)
    tk = min(ATT_TK, seq)
    nq = seq // tq
    dv = 2 * DA_HEAD_DIM
    qcol, kcol, vcol = 2 * CONV_DIM // dv, 2 * CONV_DIM // dv + DA_HEADS, 2 * CONV_DIM // dv + 2 * DA_HEADS
    tab_q = pl.BlockSpec((None, tq, LANES), lambda b, h, i: (b, i, 0))
    tab_k = pl.BlockSpec((None, seq, LANES), lambda b, h, i: (b, 0, 0))
    return pl.pallas_call(
        functools.partial(_attn_kernel, tk=tk),
        grid=(bsz, DA_HEADS, nq),
        in_specs=[pl.BlockSpec(memory_space=pltpu.SMEM),
                  pl.BlockSpec((tq, dv), lambda b, h, i: (b * nq + i, qcol + h)),
                  pl.BlockSpec((seq, dv), lambda b, h, i: (b, kcol + h)),
                  pl.BlockSpec((seq, dv), lambda b, h, i: (b, vcol + h)),
                  tab_q, tab_q, tab_k, tab_k,
                  pl.BlockSpec((4, DA_HEAD_DIM), lambda b, h, i: (0, 0)),
                  pl.BlockSpec((1, dv), lambda b, h, i: (0, 0))],
        out_specs=pl.BlockSpec((tq, dv), lambda b, h, i: (b * nq + i, h)),
        out_shape=jax.ShapeDtypeStruct((t, DA_HEADS * dv), BF16),
        scratch_shapes=[pltpu.VMEM((seq, dv), BF16), pltpu.VMEM((seq, 2 * dv), BF16),
                        pltpu.VMEM((2, tq, 2 * dv), F32), pltpu.VMEM((2, tq, 1), F32)],
        compiler_params=_cparams("parallel", "parallel", "arbitrary"),
        name="diff_attention",
    )(lam_init, z, z, z, cos_t, sin_t, cos_t, sin_t, da_lambda, subln_g.reshape(1, dv))


def _hgrn_constants():
    n = CHUNK
    t = np.arange(n)[:, None]
    u = np.arange(n)[None, :]
    incl = (u <= t).astype(np.float32)
    rest = (u > t).astype(np.float32)
    mats = [incl, rest]
    masks = []
    m = n // 2
    while m >= 1:
        ref = (np.arange(n) // (2 * m)) * (2 * m) + m - 1
        mats.append(incl - incl[ref])
        same = (t // (2 * m)) == (u // (2 * m))
        masks.append((same & ((t % (2 * m)) >= m) & ((u % (2 * m)) < m)).astype(np.float32))
        m //= 2
    masks.append(np.eye(n, dtype=np.float32))
    return np.concatenate(mats, 0), np.stack(masks, 0)


def _hgrn_kernel(q_ref, f_ref, i_ref, g_ref, lbp_ref, gn_ref, nst_ref, msk_ref, o_ref, *, layer):
    seq, dk = q_ref.shape
    n_lvl = msk_ref.shape[0] - 1
    lbp = lbp_ref[...]
    e = jnp.exp(lbp - jnp.max(lbp, axis=0, keepdims=True))
    p = e / jnp.sum(e, axis=0, keepdims=True)
    lower = jnp.zeros((1, dk), F32)
    for r in range(1, layer + 1):
        lower = lower + p[r:r + 1]
    gain = gn_ref[...]

    def body(c, st):
        r0 = pl.multiple_of(c * CHUNK, CHUNK)
        rows = pl.ds(r0, CHUNK)
        q = _silu(q_ref[rows, :].astype(F32))
        fr = f_ref[rows, :].astype(F32)
        vv = i_ref[rows, :]
        ex = jnp.exp(-jnp.abs(fr))
        rcp = 1.0 / (1.0 + ex)
        pos = fr >= 0.0
        sig = jnp.where(pos, rcp, ex * rcp)
        nsig = jnp.where(pos, ex * rcp, rcp)
        f = lower + (1.0 - lower) * sig
        lf = jnp.log(jnp.maximum(f, LOG_FLOOR))
        kk = (1.0 - lower) * nsig
        hi = lf.astype(BF16)
        lo = (lf - hi.astype(F32)).astype(BF16)
        d2 = _dot(nst_ref[...], jnp.concatenate([hi, lo], axis=1))
        dall = d2[:, :dk] + d2[:, dk:]
        b = dall[0:CHUNK]
        rest = dall[CHUNK:2 * CHUNK]
        scores = msk_ref[n_lvl] * _dot_nt(q.astype(BF16), kk.astype(BF16))
        for lv in range(n_lvl):
            xl = jnp.exp(-jnp.abs(dall[(2 + lv) * CHUNK:(3 + lv) * CHUNK]))
            scores = scores + msk_ref[lv] * _dot_nt((q * xl).astype(BF16), (kk * xl).astype(BF16))
        o = _dot(scores.astype(BF16), vv) + _dot_nt((q * jnp.exp(b)).astype(BF16), st.astype(BF16))
        khat = (kk * jnp.exp(rest)).astype(BF16)
        st_new = st * jnp.exp(b[CHUNK - 1:CHUNK]) + _dot_tn(vv, khat)
        o = o * lax.rsqrt(jnp.mean(o * o, axis=-1, keepdims=True) + NORM_EPS) * gain
        o_ref[rows, :] = (o * _silu(g_ref[rows, :].astype(F32))).astype(BF16)
        return st_new

    lax.fori_loop(0, seq // CHUNK, body, jnp.zeros((dk, dk), F32))


def _hgrn2(z, hg_lb, gnorm_g, layer, bsz, seq):
    t = z.shape[0]
    depth = hg_lb.shape[0]
    nst, msk = _hgrn_constants()
    base = (2 * CONV_DIM + 3 * DA_HEADS * 2 * DA_HEAD_DIM) // HG_DIM
    col = lambda k: pl.BlockSpec((seq, HG_DIM), lambda b, h, k=k: (b, base + k * HG_HEADS + h))
    return pl.pallas_call(
        functools.partial(_hgrn_kernel, layer=layer),
        grid=(bsz, HG_HEADS),
        in_specs=[col(0), col(1), col(2), col(3),
                  pl.BlockSpec((depth, HG_DIM), lambda b, h: (0, h)),
                  pl.BlockSpec((1, HG_DIM), lambda b, h: (0, 0)),
                  pl.BlockSpec(nst.shape, lambda b, h: (0, 0)),
                  pl.BlockSpec(msk.shape, lambda b, h: (0, 0, 0))],
        out_specs=pl.BlockSpec((seq, HG_DIM), lambda b, h: (b, h)),
        out_shape=jax.ShapeDtypeStruct((t, HG_HEADS * HG_DIM), BF16),
        compiler_params=_cparams("parallel", "parallel"),
        name="hgrn2",
    )(z, z, z, z, hg_lb, gnorm_g.reshape(1, HG_DIM), jnp.asarray(nst, BF16), jnp.asarray(msk))


def _merge_kernel(a_ref, b_ref, c_ref, ga_ref, gb_ref, gc_ref, wa_ref, wb_ref, wc_ref, o_ref):
    y = _sigmoid(ga_ref[...].astype(F32)) * _dot(a_ref[...], wa_ref[...])
    y = y + _sigmoid(gb_ref[...].astype(F32)) * _dot(b_ref[...], wb_ref[...])
    y = y + _sigmoid(gc_ref[...].astype(F32)) * _dot(c_ref[...], wc_ref[...])
    o_ref[...] = y.astype(BF16)


def _merge(ua, ub, uc, z, wa, wb, wc, seq):
    t, kdim = ua.shape
    d = wa.shape[1]
    tm = min(1024, seq)
    tn = 512
    gbase = (z.shape[1] - N_BRANCH * d) // tn
    act = pl.BlockSpec((tm, kdim), lambda i, j: (i, 0))
    gate = lambda k: pl.BlockSpec((tm, tn), lambda i, j, k=k: (i, gbase + k * (d // tn) + j))
    wsp = pl.BlockSpec((kdim, tn), lambda i, j: (0, j))
    return pl.pallas_call(
        _merge_kernel,
        grid=(t // tm, d // tn),
        in_specs=[act, act, act, gate(0), gate(1), gate(2), wsp, wsp, wsp],
        out_specs=pl.BlockSpec((tm, tn), lambda i, j: (i, j)),
        out_shape=jax.ShapeDtypeStruct((t, d), BF16),
        compiler_params=_cparams("parallel", "arbitrary"),
        name="merge",
    )(ua, ub, uc, z, z, z, wa, wb, wc)


def _oproj_kernel(m_ref, w_ref, x_ref, g_ref, o_ref):
    o_ref[...] = x_ref[...] + g_ref[...] * _dot(m_ref[...], w_ref[...])


def _out_proj(merged, w_o, x2, gate1, seq):
    t, d = x2.shape
    tm = min(1024, seq)
    tn = 1024
    per_b = seq // tm
    return pl.pallas_call(
        _oproj_kernel,
        grid=(t // tm, d // tn),
        in_specs=[pl.BlockSpec((tm, d), lambda i, j: (i, 0)),
                  pl.BlockSpec((d, tn), lambda i, j: (0, j)),
                  pl.BlockSpec((tm, tn), lambda i, j: (i, j)),
                  pl.BlockSpec((None, 1, tn), lambda i, j: (i // per_b, 0, j))],
        out_specs=pl.BlockSpec((tm, tn), lambda i, j: (i, j)),
        out_shape=jax.ShapeDtypeStruct((t, d), F32),
        compiler_params=_cparams("parallel", "arbitrary"),
        name="out_proj",
    )(merged, w_o, x2, gate1)


def _router_kernel(x_ref, g_ref, sh_ref, sc_ref, wr_ref, br_ref, h_ref, r_ref):
    tm, d = x_ref.shape
    nch = d // LANES
    h = _rms_mod(x_ref[...], g_ref[...], sh_ref[...], sc_ref[...])
    for c in range(nch):
        h_ref[pl.ds(c, tm, stride=nch), :] = h[:, c * LANES:(c + 1) * LANES]
    logits = jnp.dot(h, wr_ref[...], preferred_element_type=F32, precision=HIGHEST) + br_ref[...]
    lane = lax.broadcasted_iota(jnp.int32, logits.shape, 1)
    neg = -jnp.inf
    is_g = lane < N_GROUPS
    gl = jnp.where(is_g, logits, neg)
    gmax = jnp.max(gl, axis=-1, keepdims=True)
    gsel = jnp.min(jnp.where(gl == gmax, lane, LANES), axis=-1, keepdims=True)
    gw = 1.0 / jnp.sum(jnp.where(is_g, jnp.exp(logits - gmax), 0.0), axis=-1, keepdims=True)
    lo = N_GROUPS + gsel * EXPERTS_PER_GROUP
    el = jnp.where((lane >= lo) & (lane < lo + EXPERTS_PER_GROUP), logits, neg)
    v1 = jnp.max(el, axis=-1, keepdims=True)
    i1 = jnp.min(jnp.where(el == v1, lane, LANES), axis=-1, keepdims=True)
    el2 = jnp.where(lane == i1, neg, el)
    v2 = jnp.max(el2, axis=-1, keepdims=True)
    i2 = jnp.min(jnp.where(el2 == v2, lane, LANES), axis=-1, keepdims=True)
    tt = jnp.exp(v2 - v1)
    w1 = gw / (1.0 + tt)
    w2 = gw * tt / (1.0 + tt)
    out = jnp.where(lane == 0, (i1 - N_GROUPS).astype(F32), 0.0)
    out = jnp.where(lane == 1, (i2 - N_GROUPS).astype(F32), out)
    out = jnp.where(lane == 2, w1, out)
    r_ref[...] = jnp.where(lane == 3, w2, out)


def _router(x2, gain, shift, scale, w_r, b_r, seq):
    t, d = x2.shape
    nch = d // LANES
    tm = min(256, seq)
    per_b = seq // tm
    mod_spec = pl.BlockSpec((None, 1, d), lambda i: (i // per_b, 0, 0))
    return pl.pallas_call(
        _router_kernel,
        grid=(t // tm,),
        in_specs=[pl.BlockSpec((tm, d), lambda i: (i, 0)),
                  pl.BlockSpec((1, d), lambda i: (0, 0)),
                  mod_spec, mod_spec,
                  pl.BlockSpec((d, LANES), lambda i: (0, 0)),
                  pl.BlockSpec((1, LANES), lambda i: (0, 0))],
        out_specs=[pl.BlockSpec((tm * nch, LANES), lambda i: (i, 0)),
                   pl.BlockSpec((tm, LANES), lambda i: (i, 0))],
        out_shape=[jax.ShapeDtypeStruct((t * nch, LANES), F32),
                   jax.ShapeDtypeStruct((t, LANES), F32)],
        compiler_params=_cparams("parallel"),
        name="router",
    )(x2, gain, shift, scale, w_r, b_r)


def _routing_plan(e_idx, n_tok):
    n_assign = n_tok * TOP_K
    flat_e = e_idx.reshape(n_assign)
    order = jnp.argsort(flat_e).astype(jnp.int32)
    se = flat_e[order]
    counts = jnp.bincount(flat_e, length=N_EXPERTS).astype(jnp.int32)
    padded = ((counts + ROW_BLOCK - 1) // ROW_BLOCK) * ROW_BLOCK
    pad_end = jnp.cumsum(padded)
    pad_start = pad_end - padded
    start = jnp.cumsum(counts) - counts
    dest = pad_start[se] + jnp.arange(n_assign, dtype=jnp.int32) - start[se]
    n_blocks = -(-(n_assign + N_EXPERTS * (ROW_BLOCK - 1)) // ROW_BLOCK)
    n_slots = n_blocks * ROW_BLOCK
    slot_tok = jnp.zeros((n_slots,), jnp.int32).at[dest].set(order // TOP_K)
    slot_of = jnp.zeros((n_assign,), jnp.int32).at[order].set(dest)
    block_exp = jnp.minimum(jnp.searchsorted(pad_end, jnp.arange(n_blocks, dtype=jnp.int32) * ROW_BLOCK,
                                             side='right'), N_EXPERTS - 1).astype(jnp.int32)
    return slot_tok.reshape(n_blocks, 1, ROW_BLOCK), slot_of.reshape(n_tok, TOP_K), block_exp


def _row_copy(src_hbm, dst, sem, src_row, dst_row, nch):
    return pltpu.make_async_copy(src_hbm.at[pl.ds(pl.multiple_of(src_row * nch, nch), nch), :],
                                 dst.at[pl.ds(pl.multiple_of(dst_row * nch, nch), nch), :], sem)


def _gather_rows(idx_ref, n_rows, src_hbm, dst, sem, nch):
    def issue(r, carry):
        _row_copy(src_hbm, dst, sem, idx_ref[r], r, nch).start()
        return carry
    lax.fori_loop(0, n_rows, issue, 0)


def _wait_rows(n_rows, src_hbm, dst, sem, nch):
    pltpu.make_async_copy(src_hbm.at[pl.ds(0, n_rows * nch), :], dst.at[pl.ds(0, n_rows * nch), :], sem).wait()


def _expert_kernel(be_ref, tok_ref, nxt_ref, h_hbm, wg_ref, wu_ref, wd_ref, y_ref, xbuf, sem):
    i = pl.program_id(0)
    n = pl.num_programs(0)
    rows = tok_ref.shape[-1]
    nch = xbuf.shape[1] // rows
    slot = i % 2

    @pl.when(i == 0)
    def _():
        _gather_rows(tok_ref.at[0, 0], rows, h_hbm, xbuf.at[0], sem.at[0], nch)

    @pl.when(i + 1 < n)
    def _():
        _gather_rows(nxt_ref.at[0, 0], rows, h_hbm, xbuf.at[1 - slot], sem.at[1 - slot], nch)

    _wait_rows(rows, h_hbm, xbuf.at[slot], sem.at[slot], nch)
    xs = xbuf.at[slot]
    x = jnp.concatenate([xs[pl.ds(c, rows, stride=nch), :] for c in range(nch)], axis=1).astype(BF16)
    hid = _silu(_dot(x, wg_ref[...])) * _dot(x, wu_ref[...])
    y = _dot(hid.astype(BF16), wd_ref[...])
    for c in range(nch):
        y_ref[pl.ds(c, rows, stride=nch), :] = y[:, c * LANES:(c + 1) * LANES]


def _experts(h_rows, slot_tok, block_exp, wg, wu, wd):
    n_blocks = slot_tok.shape[0]
    _, d, hdim = wg.shape
    nch = d // LANES
    tok_spec = lambda f: pl.BlockSpec((1, 1, ROW_BLOCK), f, memory_space=pltpu.SMEM)
    grid_spec = pltpu.PrefetchScalarGridSpec(
        num_scalar_prefetch=1,
        grid=(n_blocks,),
        in_specs=[tok_spec(lambda i, be: (i, 0, 0)),
                  tok_spec(lambda i, be: (jnp.minimum(i + 1, n_blocks - 1), 0, 0)),
                  pl.BlockSpec(memory_space=pl.ANY),
                  pl.BlockSpec((None, d, hdim), lambda i, be: (be[i], 0, 0)),
                  pl.BlockSpec((None, d, hdim), lambda i, be: (be[i], 0, 0)),
                  pl.BlockSpec((None, hdim, d), lambda i, be: (be[i], 0, 0))],
        out_specs=pl.BlockSpec((ROW_BLOCK * nch, LANES), lambda i, be: (i, 0)),
        scratch_shapes=[pltpu.VMEM((2, ROW_BLOCK * nch, LANES), F32), pltpu.SemaphoreType.DMA((2,))],
    )
    return pl.pallas_call(
        _expert_kernel,
        grid_spec=grid_spec,
        out_shape=jax.ShapeDtypeStruct((n_blocks * ROW_BLOCK * nch, LANES), F32),
        compiler_params=_cparams("arbitrary"),
        name="experts",
    )(block_exp, slot_tok, slot_tok, h_rows, wg, wu, wd)


def _combine_kernel(sl_ref, nxt_ref, y_hbm, x_ref, r_ref, g_ref, o_ref, ybuf, sem):
    i = pl.program_id(0)
    n = pl.num_programs(0)
    tm, d = x_ref.shape
    nch = d // LANES
    rows = TOP_K * tm
    slot = i % 2

    @pl.when(i == 0)
    def _():
        _gather_rows(sl_ref.at[0, 0], rows, y_hbm, ybuf.at[0], sem.at[0], nch)

    @pl.when(i + 1 < n)
    def _():
        _gather_rows(nxt_ref.at[0, 0], rows, y_hbm, ybuf.at[1 - slot], sem.at[1 - slot], nch)

    _wait_rows(rows, y_hbm, ybuf.at[slot], sem.at[slot], nch)
    ys = ybuf.at[slot]
    r = r_ref[...]
    moe = jnp.zeros((tm, d), F32)
    for k in range(TOP_K):
        yk = jnp.concatenate([ys[pl.ds(k * tm * nch + c, tm, stride=nch), :] for c in range(nch)], axis=1)
        moe = moe + r[:, TOP_K + k:TOP_K + k + 1] * yk
    o_ref[...] = x_ref[...] + g_ref[...] * moe


def _combine(y_rows, slot_of, x2, route, gate2, seq):
    t, d = x2.shape
    nch = d // LANES
    tm = min(256, seq)
    nt = t // tm
    per_b = seq // tm
    slots = slot_of.reshape(nt, tm, TOP_K).transpose(0, 2, 1).reshape(nt, 1, TOP_K * tm)
    sl_spec = lambda f: pl.BlockSpec((1, 1, TOP_K * tm), f, memory_space=pltpu.SMEM)
    return pl.pallas_call(
        _combine_kernel,
        grid=(nt,),
        in_specs=[sl_spec(lambda i: (i, 0, 0)),
                  sl_spec(lambda i: (jnp.minimum(i + 1, nt - 1), 0, 0)),
                  pl.BlockSpec(memory_space=pl.ANY),
                  pl.BlockSpec((tm, d), lambda i: (i, 0)),
                  pl.BlockSpec((tm, LANES), lambda i: (i, 0)),
                  pl.BlockSpec((None, 1, d), lambda i: (i // per_b, 0, 0))],
        out_specs=pl.BlockSpec((tm, d), lambda i: (i, 0)),
        out_shape=jax.ShapeDtypeStruct((t, d), F32),
        scratch_shapes=[pltpu.VMEM((2, TOP_K * tm * nch, LANES), F32), pltpu.SemaphoreType.DMA((2,))],
        compiler_params=_cparams("arbitrary"),
        name="combine",
    )(slots, slots, y_rows, x2, route, gate2)


def _final_kernel(x_ref, g_ref, o_ref):
    x = x_ref[...]
    o_ref[...] = x * lax.rsqrt(jnp.mean(x * x, axis=-1, keepdims=True) + NORM_EPS) * g_ref[...]


def _final_norm(x2, gain):
    t, d = x2.shape
    tm = 512
    return pl.pallas_call(
        _final_kernel,
        grid=(t // tm,),
        in_specs=[pl.BlockSpec((tm, d), lambda i: (i, 0)), pl.BlockSpec((1, d), lambda i: (0, 0))],
        out_specs=pl.BlockSpec((tm, d), lambda i: (i, 0)),
        out_shape=jax.ShapeDtypeStruct((t, d), F32),
        compiler_params=_cparams("parallel"),
        name="final_norm",
    )(x2, gain.reshape(1, d))


def _mixer(x2, l, ada, cos_t, sin_t, bsz, seq, norm1_g, w_in, conv_dw, conv_dw_b, conv_ln_g, conv_ln_b,
           w_conv_out, da_lambda, da_subln_g, w_da_out, hg_lb, hg_norm_g, w_hg_out, w_o):
    d = x2.shape[1]
    part = lambda k: ada[l, :, k * d:(k + 1) * d].reshape(bsz, 1, d)
    z = _in_proj(x2, norm1_g[l].reshape(1, d), part(0), part(1), w_in[l].astype(BF16), seq)
    ua = _conv_branch(z, conv_dw[l], conv_dw_b[l], conv_ln_g[l], conv_ln_b[l], bsz, seq)
    lam_init = jnp.full((1,), 0.8 - 0.6 * math.exp(-0.3 * l), F32)
    ub = _diff_attention(z, cos_t, sin_t, da_lambda[l], da_subln_g[l], lam_init, bsz, seq)
    uc = _hgrn2(z, hg_lb, hg_norm_g[l], l, bsz, seq)
    merged = _merge(ua, ub, uc, z, w_conv_out[l].astype(BF16), w_da_out[l].astype(BF16),
                    w_hg_out[l].astype(BF16), seq)
    return _out_proj(merged, w_o[l].astype(BF16), x2, part(2), seq)


def _moe(x2, l, ada, bsz, seq, norm2_g, w_group, b_group, w_route, b_route, w_gate, w_up, w_down):
    t, d = x2.shape
    part = lambda k: ada[l, :, k * d:(k + 1) * d].reshape(bsz, 1, d)
    pad = LANES - N_GROUPS - N_EXPERTS
    w_r = jnp.concatenate([w_group[l], w_route[l], jnp.zeros((d, pad), F32)], axis=1)
    b_r = jnp.concatenate([b_group[l], b_route[l], jnp.zeros((pad,), F32)]).reshape(1, LANES)
    h_rows, route = _router(x2, norm2_g[l].reshape(1, d), part(3), part(4), w_r, b_r, seq)
    e_idx = route[:, :TOP_K].astype(jnp.int32)
    slot_tok, slot_of, block_exp = _routing_plan(e_idx, t)
    y_rows = _experts(h_rows, slot_tok, block_exp, w_gate[l].astype(BF16), w_up[l].astype(BF16),
                      w_down[l].astype(BF16))
    return _combine(y_rows, slot_of, x2, route, part(5), seq)


def kernel(x, c, positions, norm1_g, norm2_g, w_ada, b_ada, w_in, conv_dw, conv_dw_b, conv_ln_g, conv_ln_b,
           w_conv_out, da_lambda, da_subln_g, w_da_out, hg_lb, hg_norm_g, w_hg_out, w_o, w_group, b_group,
           w_route, b_route, w_gate, w_up, w_down, final_g):
    bsz, seq, d = x.shape
    depth = w_in.shape[0]
    ada = _ada(c, w_ada, b_ada)
    cos_t, sin_t = _rope_tables(positions)
    x2 = x.reshape(bsz * seq, d)
    for l in range(depth):
        x2 = _mixer(x2, l, ada, cos_t, sin_t, bsz, seq, norm1_g, w_in, conv_dw, conv_dw_b, conv_ln_g,
                    conv_ln_b, w_conv_out, da_lambda, da_subln_g, w_da_out, hg_lb, hg_norm_g, w_hg_out, w_o)
        x2 = _moe(x2, l, ada, bsz, seq, norm2_g, w_group, b_group, w_route, b_route, w_gate, w_up, w_down)
    return _final_norm(x2, final_g).reshape(bsz, seq, d)
```

```python
import functools
import math

import numpy as np
import jax
import jax.numpy as jnp
from jax import lax
from jax.experimental import pallas as pl
from jax.experimental.pallas import tpu as pltpu

F32 = jnp.float32
BF16 = jnp.bfloat16
HIGHEST = lax.Precision.HIGHEST

LANES = 128
CHUNK = 64
NORM_EPS = 1e-6
MASK_VALUE = -1e30
LOG_FLOOR = 1e-30
CONV_DIM = 1024
CONV_WIDTH = 31
CONV_HALO = 32
DA_HEADS = 8
DA_HEAD_DIM = 64
ROT_DIM = DA_HEAD_DIM // 4
ROPE_THETA = 500000.0
HG_HEADS = 8
HG_DIM = 128
HG_GROUP = 4
N_BRANCH = 3
N_GROUPS = 4
EXPERTS_PER_GROUP = 8
N_EXPERTS = N_GROUPS * EXPERTS_PER_GROUP
TOP_K = 2
ROW_BLOCK = 256
VMEM_LIMIT = 56 * 1024 * 1024


def _cparams(*sem):
    return pltpu.CompilerParams(dimension_semantics=sem, vmem_limit_bytes=VMEM_LIMIT)


def _sigmoid(v):
    return 1.0 / (1.0 + jnp.exp(-v))


def _silu(v):
    return v * _sigmoid(v)


def _dot(a, b):
    return jnp.dot(a, b, preferred_element_type=F32)


def _dot_nt(a, b):
    return lax.dot_general(a, b, (((1,), (1,)), ((), ())), preferred_element_type=F32)


def _dot_tn(a, b):
    return lax.dot_general(a, b, (((0,), (0,)), ((), ())), preferred_element_type=F32)


def _rms_mod(x, gain, shift, scale):
    ms = jnp.mean(x * x, axis=-1, keepdims=True)
    y = x * lax.rsqrt(ms + NORM_EPS) * gain
    return y * (1.0 + scale) + shift


def _ada_kernel(c_ref, w_ref, b_ref, o_ref):
    cond = _silu(c_ref[...])
    o_ref[...] = jnp.dot(cond, w_ref[...], preferred_element_type=F32, precision=HIGHEST) + b_ref[...]


def _ada(c, w_ada, b_ada):
    depth, d, cols = w_ada.shape
    bsz = c.shape[0]
    tn = 1024
    return pl.pallas_call(
        _ada_kernel,
        grid=(depth, cols // tn),
        in_specs=[pl.BlockSpec((bsz, d), lambda l, j: (0, 0)),
                  pl.BlockSpec((None, d, tn), lambda l, j: (l, 0, j)),
                  pl.BlockSpec((None, 1, tn), lambda l, j: (l, 0, j))],
        out_specs=pl.BlockSpec((None, bsz, tn), lambda l, j: (l, 0, j)),
        out_shape=jax.ShapeDtypeStruct((depth, bsz, cols), F32),
        compiler_params=_cparams("parallel", "parallel"),
        name="ada",
    )(c, w_ada, b_ada.reshape(depth, 1, cols))


def _rope_kernel(pos_ref, freq_ref, sign_ref, cos_ref, sin_ref):
    ang = pos_ref[...].astype(F32) * freq_ref[...]
    rot = sign_ref[...] != 0.0
    cos_ref[...] = jnp.where(rot, jnp.cos(ang), 1.0)
    sin_ref[...] = jnp.sin(ang) * sign_ref[...]


def _rope_tables(positions):
    bsz, seq = positions.shape
    half = ROT_DIM // 2
    inv_freq = ROPE_THETA ** (-jnp.arange(0, ROT_DIM, 2, dtype=F32) / ROT_DIM)
    lane = np.arange(LANES) % DA_HEAD_DIM
    freq = jnp.where(lane < ROT_DIM, inv_freq[lane % half], 0.0).reshape(1, LANES).astype(F32)
    sign = np.where(lane < half, -1.0, np.where(lane < ROT_DIM, 1.0, 0.0)).reshape(1, LANES).astype(np.float32)
    spec = pl.BlockSpec((None, seq, LANES), lambda b: (b, 0, 0))
    return pl.pallas_call(
        _rope_kernel,
        grid=(bsz,),
        in_specs=[pl.BlockSpec((None, seq, 1), lambda b: (b, 0, 0)),
                  pl.BlockSpec((1, LANES), lambda b: (0, 0)),
                  pl.BlockSpec((1, LANES), lambda b: (0, 0))],
        out_specs=[spec, spec],
        out_shape=[jax.ShapeDtypeStruct((bsz, seq, LANES), F32)] * 2,
        compiler_params=_cparams("parallel"),
        name="rope_tables",
    )(positions.reshape(bsz, seq, 1), freq, jnp.asarray(sign))


def _rope(t, cos, sin):
    half = ROT_DIM // 2
    lane = lax.broadcasted_iota(jnp.int32, t.shape, 1) % DA_HEAD_DIM
    partner = jnp.where(lane < half, pltpu.roll(t, LANES - half, 1), pltpu.roll(t, half, 1))
    return t * cos + partner * sin


def _inproj_kernel(x_ref, g_ref, sh_ref, sc_ref, w_ref, o_ref, h_sc):
    @pl.when(pl.program_id(1) == 0)
    def _():
        h_sc[...] = _rms_mod(x_ref[...], g_ref[...], sh_ref[...], sc_ref[...]).astype(BF16)

    o_ref[...] = _dot(h_sc[...], w_ref[...]).astype(BF16)


def _in_proj(x2, gain, shift, scale, w_bf, seq):
    t, d = x2.shape
    cols = w_bf.shape[1]
    tm = min(1024, seq)
    tn = 1024
    per_b = seq // tm
    mod_spec = pl.BlockSpec((None, 1, d), lambda i, j: (i // per_b, 0, 0))
    return pl.pallas_call(
        _inproj_kernel,
        grid=(t // tm, cols // tn),
        in_specs=[pl.BlockSpec((tm, d), lambda i, j: (i, 0)),
                  pl.BlockSpec((1, d), lambda i, j: (0, 0)),
                  mod_spec, mod_spec,
                  pl.BlockSpec((d, tn), lambda i, j: (0, j))],
        out_specs=pl.BlockSpec((tm, tn), lambda i, j: (i, j)),
        out_shape=jax.ShapeDtypeStruct((t, cols), BF16),
        scratch_shapes=[pltpu.VMEM((tm, d), BF16)],
        compiler_params=_cparams("parallel", "arbitrary"),
        name="in_proj",
    )(x2, gain, shift, scale, w_bf)


def _conv_kernel(zc_ref, zh_ref, w_ref, b_ref, lg_ref, lb_ref, o_ref, u_sc, cv_sc):
    ts = zc_ref.shape[0]
    a = zc_ref[...].astype(F32)
    u_sc[CONV_HALO:, :] = a[:, :CONV_DIM] * _sigmoid(a[:, CONV_DIM:])
    ah = zh_ref[...].astype(F32)
    uh = ah[:, :CONV_DIM] * _sigmoid(ah[:, CONV_DIM:])
    u_sc[:CONV_HALO, :] = jnp.where(pl.program_id(1) == 0, 0.0, uh)
    rc, cc = 64, 256
    off = CONV_HALO - (CONV_WIDTH - 1)
    for r0 in range(0, ts, rc):
        for c0 in range(0, CONV_DIM, cc):
            acc = jnp.broadcast_to(b_ref[:, c0:c0 + cc], (rc, cc))
            for j in range(CONV_WIDTH):
                acc = acc + w_ref[j:j + 1, c0:c0 + cc] * u_sc[r0 + off + j:r0 + off + j + rc, c0:c0 + cc]
            cv_sc[r0:r0 + rc, c0:c0 + cc] = acc
    v = cv_sc[...]
    mu = jnp.mean(v, axis=-1, keepdims=True)
    vc = v - mu
    y = vc * lax.rsqrt(jnp.mean(vc * vc, axis=-1, keepdims=True) + NORM_EPS)
    o_ref[...] = _silu(y * lg_ref[...] + lb_ref[...]).astype(BF16)


def _conv_branch(z, conv_dw, conv_b, ln_g, ln_b, bsz, seq):
    t = z.shape[0]
    ts = min(256, seq)
    per_b = seq // ts
    hb = ts // CONV_HALO
    row = lambda v: v.reshape(1, CONV_DIM)
    const = lambda shp: pl.BlockSpec(shp, lambda b, i: (0, 0))
    return pl.pallas_call(
        _conv_kernel,
        grid=(bsz, per_b),
        in_specs=[pl.BlockSpec((ts, 2 * CONV_DIM), lambda b, i: (b * per_b + i, 0)),
                  pl.BlockSpec((CONV_HALO, 2 * CONV_DIM),
                               lambda b, i: (jnp.maximum((b * per_b + i) * hb - 1, 0), 0)),
                  const((CONV_WIDTH, CONV_DIM)), const((1, CONV_DIM)),
                  const((1, CONV_DIM)), const((1, CONV_DIM))],
        out_specs=pl.BlockSpec((ts, CONV_DIM), lambda b, i: (b * per_b + i, 0)),
        out_shape=jax.ShapeDtypeStruct((t, CONV_DIM), BF16),
        scratch_shapes=[pltpu.VMEM((ts + CONV_HALO, CONV_DIM), F32), pltpu.VMEM((ts, CONV_DIM), F32)],
        compiler_params=_cparams("parallel", "arbitrary"),
        name="conv_branch",
    )(z, z, conv_dw, row(conv_b), row(ln_g), row(ln_b))


ATT_TQ = 512
ATT_TK = 512


def _attn_kernel(li_ref, q_ref, k_ref, v_ref, cq_ref, sq_ref, ck_ref, sk_ref, dl_ref, g_ref, o_ref,
                 kr_sc, va_sc, acc_sc, m_sc, *, tk):
    qi = pl.program_id(2)
    tq = q_ref.shape[0]
    dv = v_ref.shape[1]

    @pl.when(qi == 0)
    def _():
        kr_sc[...] = _rope(k_ref[...].astype(F32), ck_ref[...], sk_ref[...]).astype(BF16)
        va_sc[:, :dv] = v_ref[...]
        va_sc[:, dv:] = jnp.ones((va_sc.shape[0], dv), BF16)

    q = _rope(q_ref[...].astype(F32), cq_ref[...], sq_ref[...]) * (DA_HEAD_DIM ** -0.5)
    lane = lax.broadcasted_iota(jnp.int32, q.shape, 1)
    qc = (jnp.where(lane < DA_HEAD_DIM, q, 0.0).astype(BF16),
          jnp.where(lane >= DA_HEAD_DIM, q, 0.0).astype(BF16))
    acc_sc[...] = jnp.zeros(acc_sc.shape, F32)
    m_sc[...] = jnp.full(m_sc.shape, MASK_VALUE, F32)

    def step(j, masked):
        k0 = pl.multiple_of(j * tk, tk)
        kb = kr_sc[pl.ds(k0, tk), :]
        vb = va_sc[pl.ds(k0, tk), :]
        if masked:
            qch = (qi * tq + lax.broadcasted_iota(jnp.int32, (tq, tk), 0)) // CHUNK
            kch = (k0 + lax.broadcasted_iota(jnp.int32, (tq, tk), 1)) // CHUNK
            vis = kch <= qch
        for c in range(2):
            s = _dot_nt(qc[c], kb)
            if masked:
                s = jnp.where(vis, s, MASK_VALUE)
            m_old = m_sc[c]
            m_new = jnp.maximum(m_old, jnp.max(s, axis=-1, keepdims=True))
            p = jnp.exp(s - m_new)
            acc_sc[c] = jnp.exp(m_old - m_new) * acc_sc[c] + _dot(p.astype(BF16), vb)
            m_sc[c] = m_new

    n_full = (qi * tq) // tk

    def body(j, carry):
        step(j, False)
        return carry

    lax.fori_loop(0, n_full, body, 0)
    step(n_full, True)

    lam_init = li_ref[0]
    lv = dl_ref[...]
    lam = (jnp.exp(jnp.sum(lv[0:1] * lv[1:2], axis=-1, keepdims=True))
           - jnp.exp(jnp.sum(lv[2:3] * lv[3:4], axis=-1, keepdims=True)) + lam_init)
    a0 = acc_sc[0]
    a1 = acc_sc[1]
    o = a0[:, :dv] / a0[:, dv:dv + 1] - lam * (a1[:, :dv] / a1[:, dv:dv + 1])
    o = o * lax.rsqrt(jnp.mean(o * o, axis=-1, keepdims=True) + NORM_EPS) * g_ref[...]
    o_ref[...] = (o * (1.0 - lam_init)).astype(BF16)


def _diff_attention(z, cos_t, sin_t, da_lambda, subln_g, lam_init, bsz, seq):
    t = z.shape[0]
    tq = min(ATT_TQ, seq)
    tk = min(ATT_TK, seq)
    nq = seq // tq
    dv = 2 * DA_HEAD_DIM
    qcol, kcol, vcol = 2 * CONV_DIM // dv, 2 * CONV_DIM // dv + DA_HEADS, 2 * CONV_DIM // dv + 2 * DA_HEADS
    tab_q = pl.BlockSpec((None, tq, LANES), lambda b, h, i: (b, i, 0))
    tab_k = pl.BlockSpec((None, seq, LANES), lambda b, h, i: (b, 0, 0))
    return pl.pallas_call(
        functools.partial(_attn_kernel, tk=tk),
        grid=(bsz, DA_HEADS, nq),
        in_specs=[pl.BlockSpec(memory_space=pltpu.SMEM),
                  pl.BlockSpec((tq, dv), lambda b, h, i: (b * nq + i, qcol + h)),
                  pl.BlockSpec((seq, dv), lambda b, h, i: (b, kcol + h)),
                  pl.BlockSpec((seq, dv), lambda b, h, i: (b, vcol + h)),
                  tab_q, tab_q, tab_k, tab_k,
                  pl.BlockSpec((4, DA_HEAD_DIM), lambda b, h, i: (0, 0)),
                  pl.BlockSpec((1, dv), lambda b, h, i: (0, 0))],
        out_specs=pl.BlockSpec((tq, dv), lambda b, h, i: (b * nq + i, h)),
        out_shape=jax.ShapeDtypeStruct((t, DA_HEADS * dv), BF16),
        scratch_shapes=[pltpu.VMEM((seq, dv), BF16), pltpu.VMEM((seq, 2 * dv), BF16),
                        pltpu.VMEM((2, tq, 2 * dv), F32), pltpu.VMEM((2, tq, 1), F32)],
        compiler_params=_cparams("parallel", "parallel", "arbitrary"),
        name="diff_attention",
    )(lam_init, z, z, z, cos_t, sin_t, cos_t, sin_t, da_lambda, subln_g.reshape(1, dv))


def _hgrn_constants():
    n = CHUNK
    t = np.arange(n)[:, None]
    u = np.arange(n)[None, :]
    incl = (u <= t).astype(np.float32)
    rest = (u > t).astype(np.float32)
    mats = [incl, rest]
    masks = []
    m = n // 2
    while m >= 1:
        ref = (np.arange(n) // (2 * m)) * (2 * m) + m - 1
        mats.append(incl - incl[ref])
        same = (t // (2 * m)) == (u // (2 * m))
        masks.append((same & ((t % (2 * m)) >= m) & ((u % (2 * m)) < m)).astype(np.float32))
        m //= 2
    masks.append(np.eye(n, dtype=np.float32))
    return np.concatenate(mats, 0), np.stack(masks, 0)


def _hgrn_kernel(q_ref, f_ref, i_ref, g_ref, lbp_ref, gn_ref, nst_ref, msk_ref, o_ref, *, layer):
    seq, dk = q_ref.shape
    n_lvl = msk_ref.shape[0] - 1
    lbp = lbp_ref[...]
    e = jnp.exp(lbp - jnp.max(lbp, axis=0, keepdims=True))
    p = e / jnp.sum(e, axis=0, keepdims=True)
    lower = jnp.zeros((1, dk), F32)
    for r in range(1, layer + 1):
        lower = lower + p[r:r + 1]
    gain = gn_ref[...]

    group = HG_GROUP * CHUNK

    def body(c, st):
        r0 = pl.multiple_of(c * group, group)
        rows = pl.ds(r0, group)
        q_all = _silu(q_ref[rows, :].astype(F32))
        fr = f_ref[rows, :].astype(F32)
        v_all = i_ref[rows, :]
        ex = jnp.exp(-jnp.abs(fr))
        rcp = 1.0 / (1.0 + ex)
        pos = fr >= 0.0
        sig = jnp.where(pos, rcp, ex * rcp)
        nsig = jnp.where(pos, ex * rcp, rcp)
        f = lower + (1.0 - lower) * sig
        lf = jnp.log(jnp.maximum(f, LOG_FLOOR))
        k_all = (1.0 - lower) * nsig
        hi = lf.astype(BF16)
        lo = (lf - hi.astype(F32)).astype(BF16)
        hl = jnp.concatenate([hi, lo], axis=1)
        outs = []
        for g in range(HG_GROUP):
            sl = slice(g * CHUNK, (g + 1) * CHUNK)
            q, kk, vv = q_all[sl], k_all[sl], v_all[sl]
            d2 = _dot(nst_ref[...], hl[sl])
            dall = d2[:, :dk] + d2[:, dk:]
            b = dall[0:CHUNK]
            rest = dall[CHUNK:2 * CHUNK]
            scores = msk_ref[n_lvl] * _dot_nt(q.astype(BF16), kk.astype(BF16))
            for lv in range(n_lvl):
                xl = jnp.exp(-jnp.abs(dall[(2 + lv) * CHUNK:(3 + lv) * CHUNK]))
                scores = scores + msk_ref[lv] * _dot_nt((q * xl).astype(BF16), (kk * xl).astype(BF16))
            outs.append(_dot(scores.astype(BF16), vv) + _dot_nt((q * jnp.exp(b)).astype(BF16), st.astype(BF16)))
            khat = (kk * jnp.exp(rest)).astype(BF16)
            st = st * jnp.exp(b[CHUNK - 1:CHUNK]) + _dot_tn(vv, khat)
        o = jnp.concatenate(outs, axis=0)
        o = o * lax.rsqrt(jnp.mean(o * o, axis=-1, keepdims=True) + NORM_EPS) * gain
        o_ref[rows, :] = (o * _silu(g_ref[rows, :].astype(F32))).astype(BF16)
        return st

    lax.fori_loop(0, seq // group, body, jnp.zeros((dk, dk), F32))


def _hgrn2(z, hg_lb, gnorm_g, layer, bsz, seq):
    t = z.shape[0]
    depth = hg_lb.shape[0]
    nst, msk = _hgrn_constants()
    base = (2 * CONV_DIM + 3 * DA_HEADS * 2 * DA_HEAD_DIM) // HG_DIM
    col = lambda k: pl.BlockSpec((seq, HG_DIM), lambda b, h, k=k: (b, base + k * HG_HEADS + h))
    return pl.pallas_call(
        functools.partial(_hgrn_kernel, layer=layer),
        grid=(bsz, HG_HEADS),
        in_specs=[col(0), col(1), col(2), col(3),
                  pl.BlockSpec((depth, HG_DIM), lambda b, h: (0, h)),
                  pl.BlockSpec((1, HG_DIM), lambda b, h: (0, 0)),
                  pl.BlockSpec(nst.shape, lambda b, h: (0, 0)),
                  pl.BlockSpec(msk.shape, lambda b, h: (0, 0, 0))],
        out_specs=pl.BlockSpec((seq, HG_DIM), lambda b, h: (b, h)),
        out_shape=jax.ShapeDtypeStruct((t, HG_HEADS * HG_DIM), BF16),
        compiler_params=_cparams("parallel", "parallel"),
        name="hgrn2",
    )(z, z, z, z, hg_lb, gnorm_g.reshape(1, HG_DIM), jnp.asarray(nst, BF16), jnp.asarray(msk))


def _merge_kernel(a_ref, b_ref, c_ref, ga_ref, gb_ref, gc_ref, wa_ref, wb_ref, wc_ref, o_ref):
    y = _sigmoid(ga_ref[...].astype(F32)) * _dot(a_ref[...], wa_ref[...])
    y = y + _sigmoid(gb_ref[...].astype(F32)) * _dot(b_ref[...], wb_ref[...])
    y = y + _sigmoid(gc_ref[...].astype(F32)) * _dot(c_ref[...], wc_ref[...])
    o_ref[...] = y.astype(BF16)


def _merge(ua, ub, uc, z, wa, wb, wc, seq):
    t, kdim = ua.shape
    d = wa.shape[1]
    tm = min(1024, seq)
    tn = 512
    gbase = (z.shape[1] - N_BRANCH * d) // tn
    act = pl.BlockSpec((tm, kdim), lambda i, j: (i, 0))
    gate = lambda k: pl.BlockSpec((tm, tn), lambda i, j, k=k: (i, gbase + k * (d // tn) + j))
    wsp = pl.BlockSpec((kdim, tn), lambda i, j: (0, j))
    return pl.pallas_call(
        _merge_kernel,
        grid=(t // tm, d // tn),
        in_specs=[act, act, act, gate(0), gate(1), gate(2), wsp, wsp, wsp],
        out_specs=pl.BlockSpec((tm, tn), lambda i, j: (i, j)),
        out_shape=jax.ShapeDtypeStruct((t, d), BF16),
        compiler_params=_cparams("parallel", "arbitrary"),
        name="merge",
    )(ua, ub, uc, z, z, z, wa, wb, wc)


def _oproj_kernel(m_ref, w_ref, x_ref, g_ref, o_ref):
    o_ref[...] = x_ref[...] + g_ref[...] * _dot(m_ref[...], w_ref[...])


def _out_proj(merged, w_o, x2, gate1, seq):
    t, d = x2.shape
    tm = min(1024, seq)
    tn = 1024
    per_b = seq // tm
    return pl.pallas_call(
        _oproj_kernel,
        grid=(t // tm, d // tn),
        in_specs=[pl.BlockSpec((tm, d), lambda i, j: (i, 0)),
                  pl.BlockSpec((d, tn), lambda i, j: (0, j)),
                  pl.BlockSpec((tm, tn), lambda i, j: (i, j)),
                  pl.BlockSpec((None, 1, tn), lambda i, j: (i // per_b, 0, j))],
        out_specs=pl.BlockSpec((tm, tn), lambda i, j: (i, j)),
        out_shape=jax.ShapeDtypeStruct((t, d), F32),
        compiler_params=_cparams("parallel", "arbitrary"),
        name="out_proj",
    )(merged, w_o, x2, gate1)


def _router_kernel(x_ref, g_ref, sh_ref, sc_ref, wr_ref, br_ref, h_ref, r_ref):
    tm, d = x_ref.shape
    nch = d // LANES
    h = _rms_mod(x_ref[...], g_ref[...], sh_ref[...], sc_ref[...])
    for c in range(nch):
        h_ref[pl.ds(c, tm, stride=nch), :] = h[:, c * LANES:(c + 1) * LANES]
    logits = jnp.dot(h, wr_ref[...], preferred_element_type=F32, precision=HIGHEST) + br_ref[...]
    lane = lax.broadcasted_iota(jnp.int32, logits.shape, 1)
    neg = -jnp.inf
    is_g = lane < N_GROUPS
    gl = jnp.where(is_g, logits, neg)
    gmax = jnp.max(gl, axis=-1, keepdims=True)
    gsel = jnp.min(jnp.where(gl == gmax, lane, LANES), axis=-1, keepdims=True)
    gw = 1.0 / jnp.sum(jnp.where(is_g, jnp.exp(logits - gmax), 0.0), axis=-1, keepdims=True)
    lo = N_GROUPS + gsel * EXPERTS_PER_GROUP
    el = jnp.where((lane >= lo) & (lane < lo + EXPERTS_PER_GROUP), logits, neg)
    v1 = jnp.max(el, axis=-1, keepdims=True)
    i1 = jnp.min(jnp.where(el == v1, lane, LANES), axis=-1, keepdims=True)
    el2 = jnp.where(lane == i1, neg, el)
    v2 = jnp.max(el2, axis=-1, keepdims=True)
    i2 = jnp.min(jnp.where(el2 == v2, lane, LANES), axis=-1, keepdims=True)
    tt = jnp.exp(v2 - v1)
    w1 = gw / (1.0 + tt)
    w2 = gw * tt / (1.0 + tt)
    out = jnp.where(lane == 0, (i1 - N_GROUPS).astype(F32), 0.0)
    out = jnp.where(lane == 1, (i2 - N_GROUPS).astype(F32), out)
    out = jnp.where(lane == 2, w1, out)
    r_ref[...] = jnp.where(lane == 3, w2, out)


def _router(x2, gain, shift, scale, w_r, b_r, seq):
    t, d = x2.shape
    nch = d // LANES
    tm = min(256, seq)
    per_b = seq // tm
    mod_spec = pl.BlockSpec((None, 1, d), lambda i: (i // per_b, 0, 0))
    return pl.pallas_call(
        _router_kernel,
        grid=(t // tm,),
        in_specs=[pl.BlockSpec((tm, d), lambda i: (i, 0)),
                  pl.BlockSpec((1, d), lambda i: (0, 0)),
                  mod_spec, mod_spec,
                  pl.BlockSpec((d, LANES), lambda i: (0, 0)),
                  pl.BlockSpec((1, LANES), lambda i: (0, 0))],
        out_specs=[pl.BlockSpec((tm * nch, LANES), lambda i: (i, 0)),
                   pl.BlockSpec((tm, LANES), lambda i: (i, 0))],
        out_shape=[jax.ShapeDtypeStruct((t * nch, LANES), F32),
                   jax.ShapeDtypeStruct((t, LANES), F32)],
        compiler_params=_cparams("parallel"),
        name="router",
    )(x2, gain, shift, scale, w_r, b_r)


def _routing_plan(e_idx, n_tok):
    n_assign = n_tok * TOP_K
    flat_e = e_idx.reshape(n_assign)
    iota = jnp.arange(n_assign, dtype=jnp.int32)
    se, order = lax.sort((flat_e, iota), num_keys=1)
    experts = jnp.arange(N_EXPERTS, dtype=jnp.int32)
    counts = jnp.sum((flat_e[:, None] == experts[None, :]).astype(jnp.int32), axis=0)
    padded = ((counts + ROW_BLOCK - 1) // ROW_BLOCK) * ROW_BLOCK
    pad_end = jnp.cumsum(padded)
    pad_start = pad_end - padded
    start = jnp.cumsum(counts) - counts
    dest = pad_start[se] + iota - start[se]
    _, slot_of = lax.sort((order, dest), num_keys=1)
    n_blocks = -(-(n_assign + N_EXPERTS * (ROW_BLOCK - 1)) // ROW_BLOCK)
    first_row = jnp.arange(n_blocks, dtype=jnp.int32) * ROW_BLOCK
    block_exp = jnp.minimum(jnp.sum((pad_end[None, :] <= first_row[:, None]).astype(jnp.int32), axis=1),
                            N_EXPERTS - 1)
    slot = jnp.arange(n_blocks * ROW_BLOCK, dtype=jnp.int32)
    slot_e = jnp.repeat(block_exp, ROW_BLOCK)
    rank = slot - pad_start[slot_e]
    src = jnp.clip(start[slot_e] + rank, 0, n_assign - 1)
    slot_tok = jnp.where(rank < counts[slot_e], order[src] // TOP_K, 0)
    return slot_tok.reshape(n_blocks, 1, ROW_BLOCK), slot_of.reshape(n_tok, TOP_K), block_exp


SUBLANES = 8
DMA_UNROLL = 8


def _pitch(nch):
    return nch + SUBLANES if (nch // SUBLANES) % 2 == 0 else nch


def _row_copy(src_hbm, dst, sem, src_row, dst_row, nch):
    pitch = _pitch(nch)
    return pltpu.make_async_copy(src_hbm.at[pl.ds(pl.multiple_of(src_row * nch, nch), nch), :],
                                 dst.at[pl.ds(pl.multiple_of(dst_row * pitch, SUBLANES), nch), :], sem)


def _gather_rows(idx_ref, n_rows, src_hbm, dst, sem, nch):
    def issue(r, carry):
        for u in range(DMA_UNROLL):
            row = r * DMA_UNROLL + u
            _row_copy(src_hbm, dst, sem, idx_ref[row], row, nch).start()
        return carry
    lax.fori_loop(0, n_rows // DMA_UNROLL, issue, 0)


def _wait_rows(n_rows, src_hbm, dst, sem, nch):
    pltpu.make_async_copy(src_hbm.at[pl.ds(0, n_rows * nch), :], dst.at[pl.ds(0, n_rows * nch), :], sem).wait()


def _expert_kernel(be_ref, tok_ref, nxt_ref, h_hbm, wg_ref, wu_ref, wd_ref, y_ref, xbuf, wg_sc, wu_sc, wd_sc, sem):
    i = pl.program_id(0)
    n = pl.num_programs(0)
    rows = tok_ref.shape[-1]
    nch = y_ref.shape[0] // rows
    slot = i % 2

    @pl.when(i == 0)
    def _():
        _gather_rows(tok_ref.at[0, 0], rows, h_hbm, xbuf.at[0], sem.at[0], nch)

    @pl.when(i + 1 < n)
    def _():
        _gather_rows(nxt_ref.at[0, 0], rows, h_hbm, xbuf.at[1 - slot], sem.at[1 - slot], nch)

    @pl.when((i == 0) | (be_ref[i] != be_ref[jnp.maximum(i - 1, 0)]))
    def _():
        wg_sc[...] = wg_ref[...].astype(BF16)
        wu_sc[...] = wu_ref[...].astype(BF16)
        wd_sc[...] = wd_ref[...].astype(BF16)

    _wait_rows(rows, h_hbm, xbuf.at[slot], sem.at[slot], nch)
    xs = xbuf.at[slot]
    x = jnp.concatenate([xs[pl.ds(c, rows, stride=_pitch(nch)), :] for c in range(nch)], axis=1).astype(BF16)
    hid = _silu(_dot(x, wg_sc[...])) * _dot(x, wu_sc[...])
    y = _dot(hid.astype(BF16), wd_sc[...])
    for c in range(nch):
        y_ref[pl.ds(c, rows, stride=nch), :] = y[:, c * LANES:(c + 1) * LANES]


def _experts(h_rows, slot_tok, block_exp, wg, wu, wd):
    n_blocks = slot_tok.shape[0]
    _, d, hdim = wg.shape
    nch = d // LANES
    tok_spec = lambda f: pl.BlockSpec((1, 1, ROW_BLOCK), f, memory_space=pltpu.SMEM)
    grid_spec = pltpu.PrefetchScalarGridSpec(
        num_scalar_prefetch=1,
        grid=(n_blocks,),
        in_specs=[tok_spec(lambda i, be: (i, 0, 0)),
                  tok_spec(lambda i, be: (jnp.minimum(i + 1, n_blocks - 1), 0, 0)),
                  pl.BlockSpec(memory_space=pl.ANY),
                  pl.BlockSpec((None, d, hdim), lambda i, be: (be[i], 0, 0)),
                  pl.BlockSpec((None, d, hdim), lambda i, be: (be[i], 0, 0)),
                  pl.BlockSpec((None, hdim, d), lambda i, be: (be[i], 0, 0))],
        out_specs=pl.BlockSpec((ROW_BLOCK * nch, LANES), lambda i, be: (i, 0)),
        scratch_shapes=[pltpu.VMEM((2, ROW_BLOCK * _pitch(nch), LANES), F32),
                        pltpu.VMEM((d, hdim), BF16), pltpu.VMEM((d, hdim), BF16), pltpu.VMEM((hdim, d), BF16),
                        pltpu.SemaphoreType.DMA((2,))],
    )
    return pl.pallas_call(
        _expert_kernel,
        grid_spec=grid_spec,
        out_shape=jax.ShapeDtypeStruct((n_blocks * ROW_BLOCK * nch, LANES), F32),
        compiler_params=_cparams("arbitrary"),
        name="experts",
    )(block_exp, slot_tok, slot_tok, h_rows, wg, wu, wd)


def _combine_kernel(sl_ref, nxt_ref, y_hbm, x_ref, r_ref, g_ref, o_ref, ybuf, sem):
    i = pl.program_id(0)
    n = pl.num_programs(0)
    tm, d = x_ref.shape
    nch = d // LANES
    rows = TOP_K * tm
    slot = i % 2

    @pl.when(i == 0)
    def _():
        _gather_rows(sl_ref.at[0, 0], rows, y_hbm, ybuf.at[0], sem.at[0], nch)

    @pl.when(i + 1 < n)
    def _():
        _gather_rows(nxt_ref.at[0, 0], rows, y_hbm, ybuf.at[1 - slot], sem.at[1 - slot], nch)

    _wait_rows(rows, y_hbm, ybuf.at[slot], sem.at[slot], nch)
    ys = ybuf.at[slot]
    r = r_ref[...]
    wk = [jnp.broadcast_to(r[:, TOP_K + k:TOP_K + k + 1], (tm, LANES)) for k in range(TOP_K)]
    pitch = _pitch(nch)
    for c in range(nch):
        cols = slice(c * LANES, (c + 1) * LANES)
        moe = wk[0] * ys[pl.ds(c, tm, stride=pitch), :]
        for k in range(1, TOP_K):
            moe = moe + wk[k] * ys[pl.ds(k * tm * pitch + c, tm, stride=pitch), :]
        o_ref[:, cols] = x_ref[:, cols] + g_ref[:, cols] * moe


def _combine(y_rows, slot_of, x2, route, gate2, seq):
    t, d = x2.shape
    nch = d // LANES
    tm = min(256, seq)
    nt = t // tm
    per_b = seq // tm
    slots = slot_of.reshape(nt, tm, TOP_K).transpose(0, 2, 1).reshape(nt, 1, TOP_K * tm)
    sl_spec = lambda f: pl.BlockSpec((1, 1, TOP_K * tm), f, memory_space=pltpu.SMEM)
    return pl.pallas_call(
        _combine_kernel,
        grid=(nt,),
        in_specs=[sl_spec(lambda i: (i, 0, 0)),
                  sl_spec(lambda i: (jnp.minimum(i + 1, nt - 1), 0, 0)),
                  pl.BlockSpec(memory_space=pl.ANY),
                  pl.BlockSpec((tm, d), lambda i: (i, 0)),
                  pl.BlockSpec((tm, LANES), lambda i: (i, 0)),
                  pl.BlockSpec((None, 1, d), lambda i: (i // per_b, 0, 0))],
        out_specs=pl.BlockSpec((tm, d), lambda i: (i, 0)),
        out_shape=jax.ShapeDtypeStruct((t, d), F32),
        scratch_shapes=[pltpu.VMEM((2, TOP_K * tm * _pitch(nch), LANES), F32), pltpu.SemaphoreType.DMA((2,))],
        compiler_params=_cparams("arbitrary"),
        name="combine",
    )(slots, slots, y_rows, x2, route, gate2)


def _final_kernel(x_ref, g_ref, o_ref):
    x = x_ref[...]
    o_ref[...] = x * lax.rsqrt(jnp.mean(x * x, axis=-1, keepdims=True) + NORM_EPS) * g_ref[...]


def _final_norm(x2, gain):
    t, d = x2.shape
    tm = 512
    return pl.pallas_call(
        _final_kernel,
        grid=(t // tm,),
        in_specs=[pl.BlockSpec((tm, d), lambda i: (i, 0)), pl.BlockSpec((1, d), lambda i: (0, 0))],
        out_specs=pl.BlockSpec((tm, d), lambda i: (i, 0)),
        out_shape=jax.ShapeDtypeStruct((t, d), F32),
        compiler_params=_cparams("parallel"),
        name="final_norm",
    )(x2, gain.reshape(1, d))


def _mixer(x2, l, ada, cos_t, sin_t, bsz, seq, norm1_g, w_in, conv_dw, conv_dw_b, conv_ln_g, conv_ln_b,
           w_conv_out, da_lambda, da_subln_g, w_da_out, hg_lb, hg_norm_g, w_hg_out, w_o):
    d = x2.shape[1]
    part = lambda k: ada[l, :, k * d:(k + 1) * d].reshape(bsz, 1, d)
    z = _in_proj(x2, norm1_g[l].reshape(1, d), part(0), part(1), w_in[l].astype(BF16), seq)
    ua = _conv_branch(z, conv_dw[l], conv_dw_b[l], conv_ln_g[l], conv_ln_b[l], bsz, seq)
    lam_init = jnp.full((1,), 0.8 - 0.6 * math.exp(-0.3 * l), F32)
    ub = _diff_attention(z, cos_t, sin_t, da_lambda[l], da_subln_g[l], lam_init, bsz, seq)
    uc = _hgrn2(z, hg_lb, hg_norm_g[l], l, bsz, seq)
    merged = _merge(ua, ub, uc, z, w_conv_out[l].astype(BF16), w_da_out[l].astype(BF16),
                    w_hg_out[l].astype(BF16), seq)
    return _out_proj(merged, w_o[l].astype(BF16), x2, part(2), seq)


def _moe(x2, l, ada, bsz, seq, norm2_g, w_group, b_group, w_route, b_route, w_gate, w_up, w_down):
    t, d = x2.shape
    part = lambda k: ada[l, :, k * d:(k + 1) * d].reshape(bsz, 1, d)
    pad = LANES - N_GROUPS - N_EXPERTS
    w_r = jnp.concatenate([w_group[l], w_route[l], jnp.zeros((d, pad), F32)], axis=1)
    b_r = jnp.concatenate([b_group[l], b_route[l], jnp.zeros((pad,), F32)]).reshape(1, LANES)
    h_rows, route = _router(x2, norm2_g[l].reshape(1, d), part(3), part(4), w_r, b_r, seq)
    e_idx = route[:, :TOP_K].astype(jnp.int32)
    slot_tok, slot_of, block_exp = _routing_plan(e_idx, t)
    y_rows = _experts(h_rows, slot_tok, block_exp, w_gate[l], w_up[l], w_down[l])
    return _combine(y_rows, slot_of, x2, route, part(5), seq)


def kernel(x, c, positions, norm1_g, norm2_g, w_ada, b_ada, w_in, conv_dw, conv_dw_b, conv_ln_g, conv_ln_b,
           w_conv_out, da_lambda, da_subln_g, w_da_out, hg_lb, hg_norm_g, w_hg_out, w_o, w_group, b_group,
           w_route, b_route, w_gate, w_up, w_down, final_g):
    bsz, seq, d = x.shape
    depth = w_in.shape[0]
    ada = _ada(c, w_ada, b_ada)
    cos_t, sin_t = _rope_tables(positions)
    x2 = x.reshape(bsz * seq, d)
    for l in range(depth):
        x2 = _mixer(x2, l, ada, cos_t, sin_t, bsz, seq, norm1_g, w_in, conv_dw, conv_dw_b, conv_ln_g,
                    conv_ln_b, w_conv_out, da_lambda, da_subln_g, w_da_out, hg_lb, hg_norm_g, w_hg_out, w_o)
        x2 = _moe(x2, l, ada, bsz, seq, norm2_g, w_group, b_group, w_route, b_route, w_gate, w_up, w_down)
    return _final_norm(x2, final_g).reshape(bsz, seq, d)
```
